```python
import math
import jax, jax.numpy as jnp
from jax import lax
import numpy as np

D_MODEL = 4096
BATCH = 1
SEQ = 8192
DEPTH = 4

N_BRANCH = 4
BR_W = D_MODEL // 4
MEM_LEN = 256

HG_DK = 128
HG_HEADS = BR_W // HG_DK
HG_DV = BR_W // HG_HEADS
HG_CHUNK = 64

DF_DH = 64
DF_HEADS = BR_W // (2 * DF_DH)
DF_QBLOCK = 128

RW_N = 64
RW_HEADS = BR_W // RW_N
RW_DECAY_LORA = max(32, int(round((BR_W ** 0.5) * 1.8 / 32)) * 32)
RW_AAA_LORA = max(32, int(round((BR_W ** 0.5) * 1.8 / 32)) * 32)
RW_GATE_LORA = max(32, int(round((BR_W ** 0.8) * 0.6 / 32)) * 32)
RW_LORA_COLS = RW_DECAY_LORA + RW_AAA_LORA + RW_GATE_LORA
RW_COLS = 3 * BR_W + RW_LORA_COLS
RW_LNX_EPS = 64e-5

XA_HEADS = 4
XA_DH = BR_W // XA_HEADS

GATE_RANK = 256

IN_COLS = (2 * HG_HEADS * HG_DK + 2 * HG_HEADS * HG_DV) + 3 * BR_W + RW_COLS + BR_W + GATE_RANK

N_EXPERTS = 32
TOP_K = 4
EXPERT_FF = 256
SWIGLU_LIMIT = 7.0
SWIGLU_ALPHA = 1.702

DEEPNORM_ALPHA = (2 * DEPTH) ** 0.25
DEEPNORM_BETA = (8 * DEPTH) ** -0.25
LN_EPS = 1e-5

kernel_name = "hybrid_hgrn2_diffattn_rwkv7_memxattn_moe"


def _split(a, widths):
    out, off = [], 0
    for w in widths:
        out.append(a[..., off:off + w])
        off += w
    return out


def _layer_norm(x, g, b):
    xf = x.astype(jnp.float32)
    mu = jnp.mean(xf, axis=-1, keepdims=True)
    var = jnp.mean(jnp.square(xf - mu), axis=-1, keepdims=True)
    return ((xf - mu) * lax.rsqrt(var + LN_EPS) * g + b).astype(x.dtype)


def _rms_norm(x, g, eps):
    xf = x.astype(jnp.float32)
    return xf * lax.rsqrt(jnp.mean(xf * xf, axis=-1, keepdims=True) + eps) * g


def _hgrn2(q_raw, f_raw, i_in, og, lb, norm_g):
    B, T, _ = q_raw.shape
    nc = T // HG_CHUNK
    lbf = lb.astype(jnp.float32)
    z = f_raw.astype(jnp.float32)
    log_f = jnp.logaddexp(jnp.log(lbf), jnp.log1p(-lbf) + jax.nn.log_sigmoid(z))
    k = (1.0 - lbf) * jax.nn.sigmoid(-z)
    q = jax.nn.silu(q_raw.astype(jnp.float32))
    v = i_in.astype(jnp.float32)

    def to_chunks(a, d):
        return a.reshape(B, nc, HG_CHUNK, HG_HEADS, d).transpose(1, 0, 3, 2, 4)

    qc, kc, gc, vc = to_chunks(q, HG_DK), to_chunks(k, HG_DK), to_chunks(log_f, HG_DK), to_chunks(v, HG_DV)
    causal = jnp.tril(jnp.ones((HG_CHUNK, HG_CHUNK), bool))[:, :, None]

    def step(S, inp):
        qb, kb, gb, vb = inp
        b = jnp.cumsum(gb, axis=-2)
        o_inter = jnp.einsum('bhck,bhkv->bhcv', qb * jnp.exp(b), S)
        diff = b[:, :, :, None, :] - b[:, :, None, :, :]
        decay = jnp.exp(jnp.where(causal, diff, -jnp.inf))
        att = jnp.einsum('bhtk,bhtsk,bhsk->bhts', qb, decay, kb)
        o = o_inter + jnp.einsum('bhts,bhsv->bhtv', att, vb)
        b_last = b[:, :, -1:, :]
        S = jnp.exp(b_last[:, :, 0, :])[..., None] * S + jnp.einsum('bhsk,bhsv->bhkv', kb * jnp.exp(b_last - b), vb)
        return S, o

    S0 = jnp.zeros((B, HG_HEADS, HG_DK, HG_DV), jnp.float32)
    _, o = lax.scan(step, S0, (qc, kc, gc, vc))
    o = o.transpose(1, 0, 3, 2, 4).reshape(B, T, HG_HEADS, HG_DV)
    gate = jax.nn.silu(og.astype(jnp.float32)).reshape(B, T, HG_HEADS, HG_DV)
    o = _rms_norm(o, norm_g, 1e-5) * gate
    return o.reshape(B, T, BR_W).astype(i_in.dtype)


def _diff_attn(q, k, v, lam, lam_init, subln_g):
    B, T, _ = q.shape
    scale = DF_DH ** -0.5
    q = q.reshape(B, T, DF_HEADS, 2, DF_DH).transpose(0, 2, 3, 1, 4)
    k = k.reshape(B, T, DF_HEADS, 2, DF_DH).transpose(0, 2, 3, 1, 4)
    v = v.reshape(B, T, DF_HEADS, 2 * DF_DH).transpose(0, 2, 1, 3)
    nb = T // DF_QBLOCK
    q_blocks = q.reshape(B, DF_HEADS, 2, nb, DF_QBLOCK, DF_DH).transpose(3, 0, 1, 2, 4, 5)
    kpos = jnp.arange(T)

    def block(args):
        qb, start = args
        s = jnp.einsum('bhmqd,bhmkd->bhmqk', qb, k).astype(jnp.float32) * scale
        qpos = start + jnp.arange(DF_QBLOCK)
        mask = kpos[None, :] <= qpos[:, None]
        p = jax.nn.softmax(jnp.where(mask, s, -jnp.inf), axis=-1)
        a = p[:, :, 0] - lam * p[:, :, 1]
        return jnp.einsum('bhqk,bhkv->bhqv', a.astype(v.dtype), v)

    starts = jnp.arange(nb) * DF_QBLOCK
    o = lax.map(block, (q_blocks, starts))
    o = o.transpose(1, 0, 3, 2, 4).reshape(B, T, DF_HEADS, 2 * DF_DH)
    o = _rms_norm(o, subln_g, 1e-5) * (1.0 - lam_init)
    return o.reshape(B, T, BR_W).astype(v.dtype)


def _rwkv7(streams, shift_mu, w0, w2, a0, a2, g2, k_k, k_a, r_k, lnx_g, lnx_b):
    f32 = jnp.float32
    B, T, _ = streams.shape
    prev = jnp.pad(streams, ((0, 0), (1, 0), (0, 0)))[:, :-1]
    xs = streams + (prev - streams) * shift_mu
    r, k, v, wd, ad, gd = _split(xs, [BR_W, BR_W, BR_W, RW_DECAY_LORA, RW_AAA_LORA, RW_GATE_LORA])
    w_log = -jax.nn.softplus(-(w0 + jnp.tanh(wd) @ w2).astype(f32)) - 0.5
    decay = jnp.exp(-jnp.exp(w_log))
    a = jax.nn.sigmoid((a0 + ad @ a2).astype(f32))
    g = jax.nn.sigmoid(gd) @ g2
    heads = lambda t: t.astype(f32).reshape(B, T, RW_HEADS, RW_N)
    kk = heads(k * k_k)
    kk = kk / jnp.maximum(jnp.sqrt(jnp.sum(kk * kk, axis=-1, keepdims=True)), 1e-12)
    k = k.astype(f32) * (1.0 + (a - 1.0) * k_a)
    rh, kh, vh, wh, ah = heads(r), heads(k), heads(v), heads(decay), heads(a)

    def step(S, inp):
        r_t, w_t, k_t, v_t, kk_t, a_t = inp
        sa = jnp.einsum('bhvk,bhk->bhv', S, -kk_t)
        S = S * w_t[:, :, None, :] + sa[..., None] * (kk_t * a_t)[:, :, None, :] + v_t[..., None] * k_t[:, :, None, :]
        return S, jnp.einsum('bhvk,bhk->bhv', S, r_t)

    tmaj = lambda t: t.transpose(1, 0, 2, 3)
    S0 = jnp.zeros((B, RW_HEADS, RW_N, RW_N), f32)
    _, o = lax.scan(step, S0, (tmaj(rh), tmaj(wh), tmaj(kh), tmaj(vh), tmaj(kk), tmaj(ah)))
    o = o.transpose(1, 0, 2, 3)
    mu = jnp.mean(o, axis=-1, keepdims=True)
    var = jnp.mean(jnp.square(o - mu), axis=-1, keepdims=True)
    o = ((o - mu) * lax.rsqrt(var + RW_LNX_EPS)).reshape(B, T, BR_W) * lnx_g + lnx_b
    bonus = jnp.sum(rh * kh * r_k, axis=-1, keepdims=True) * vh
    o = (o + bonus.reshape(B, T, BR_W)) * g
    return o.astype(streams.dtype)


def _mem_attn(q, mem_k, mem_v):
    B, T, _ = q.shape
    M = mem_k.shape[1]
    qh = q.reshape(B, T, XA_HEADS, XA_DH)
    kh = mem_k.reshape(B, M, XA_HEADS, XA_DH)
    vh = mem_v.reshape(B, M, XA_HEADS, XA_DH)
    s = jnp.einsum('bthd,bmhd->bhtm', qh, kh).astype(jnp.float32) * (XA_DH ** -0.5)
    p = jax.nn.softmax(s, axis=-1)
    o = jnp.einsum('bhtm,bmhd->bthd', p.astype(vh.dtype), vh)
    return o.reshape(B, T, BR_W)


def _moe(h, router_w, router_b, w1, b1, w2, b2):
    B, T, D = h.shape
    t = h.reshape(B * T, D)
    logits = (t @ router_w + router_b).astype(jnp.float32)
    top_v, top_i = lax.top_k(logits, TOP_K)
    top_p = jax.nn.softmax(top_v, axis=-1)
    gates = jnp.sum(jax.nn.one_hot(top_i, N_EXPERTS, dtype=jnp.float32) * top_p[..., None], axis=1)
    hid = jnp.einsum('nd,edf->nef', t, w1) + b1
    x_glu = jnp.minimum(hid[..., ::2], SWIGLU_LIMIT)
    x_lin = jnp.clip(hid[..., 1::2], -SWIGLU_LIMIT, SWIGLU_LIMIT)
    act = x_glu * jax.nn.sigmoid(SWIGLU_ALPHA * x_glu) * (x_lin + 1.0)
    out = jnp.einsum('nef,efd->nd', act * gates[..., None], w2) + gates @ b2
    return out.reshape(B, T, D)


def setup_inputs(seed: int = 0) -> dict:
    key = jax.random.key(seed)
    ks = iter(jax.random.split(key, 48))
    f32 = jnp.float32

    def nrm(shape, scale):
        return jax.random.normal(next(ks), shape, f32) * scale

    def gain(shape):
        return 1.0 + 0.02 * jax.random.normal(next(ks), shape, f32)

    L, D = DEPTH, D_MODEL
    return {
        "x": nrm((BATCH, SEQ, D), 1.0),
        "mem": nrm((BATCH, MEM_LEN, D), 1.0),
        "mem_ln_g": gain((D,)),
        "mem_ln_b": nrm((D,), 0.02),
        "w_in": nrm((L, D, IN_COLS), D ** -0.5),
        "hg_lb_raw": nrm((L, HG_HEADS * HG_DK), 1.0),
        "hg_norm_g": gain((L, HG_DV)),
        "df_lam_q1": nrm((L, DF_DH), 0.1),
        "df_lam_k1": nrm((L, DF_DH), 0.1),
        "df_lam_q2": nrm((L, DF_DH), 0.1),
        "df_lam_k2": nrm((L, DF_DH), 0.1),
        "df_subln_g": gain((L, 2 * DF_DH)),
        "rw_shift_mu": jax.random.uniform(next(ks), (L, RW_COLS), f32),
        "rw_w0": nrm((L, BR_W), 0.5),
        "rw_w2": nrm((L, RW_DECAY_LORA, BR_W), RW_DECAY_LORA ** -0.5),
        "rw_a0": nrm((L, BR_W), 0.1),
        "rw_a2": nrm((L, RW_AAA_LORA, BR_W), RW_AAA_LORA ** -0.5),
        "rw_g2": nrm((L, RW_GATE_LORA, BR_W), RW_GATE_LORA ** -0.5),
        "rw_k_k": 0.85 + nrm((L, BR_W), 0.02),
        "rw_k_a": gain((L, BR_W)),
        "rw_r_k": nrm((L, RW_HEADS, RW_N), 0.1),
        "rw_lnx_g": gain((L, BR_W)),
        "rw_lnx_b": nrm((L, BR_W), 0.02),
        "w_mem_kv": nrm((L, D, 2 * BR_W), D ** -0.5),
        "w_br": nrm((L, N_BRANCH, BR_W, D), (BR_W ** -0.5) * DEEPNORM_BETA),
        "w_gate_up": nrm((L, GATE_RANK, N_BRANCH * D), GATE_RANK ** -0.5),
        "b_gate": nrm((L, N_BRANCH * D), 0.02),
        "w_o": nrm((L, D, D), (D ** -0.5) * DEEPNORM_BETA),
        "ln1_g": gain((L, D)),
        "ln1_b": nrm((L, D), 0.02),
        "router_w": nrm((L, D, N_EXPERTS), D ** -0.5),
        "router_b": nrm((L, N_EXPERTS), 0.01),
        "exp_w1": nrm((L, N_EXPERTS, D, 2 * EXPERT_FF), D ** -0.5),
        "exp_b1": nrm((L, N_EXPERTS, 2 * EXPERT_FF), 0.02),
        "exp_w2": nrm((L, N_EXPERTS, EXPERT_FF, D), (EXPERT_FF ** -0.5) * DEEPNORM_BETA),
        "exp_b2": nrm((L, N_EXPERTS, D), 0.02),
        "ln2_g": gain((L, D)),
        "ln2_b": nrm((L, D), 0.02),
    }


def reference(x, mem, mem_ln_g, mem_ln_b, w_in, hg_lb_raw, hg_norm_g, df_lam_q1, df_lam_k1, df_lam_q2, df_lam_k2,
              df_subln_g, rw_shift_mu, rw_w0, rw_w2, rw_a0, rw_a2, rw_g2, rw_k_k, rw_k_a, rw_r_k, rw_lnx_g, rw_lnx_b,
              w_mem_kv, w_br, w_gate_up, b_gate, w_o, ln1_g, ln1_b, router_w, router_b, exp_w1, exp_b1, exp_w2,
              exp_b2, ln2_g, ln2_b):
    f32 = jnp.float32
    memn = _layer_norm(mem, mem_ln_g, mem_ln_b)
    lb_cum = jnp.cumsum(jax.nn.softmax(hg_lb_raw.astype(f32), axis=0), axis=0)
    lb_all = lb_cum - lb_cum[:1]
    in_widths = [HG_HEADS * HG_DK, HG_HEADS * HG_DK, HG_HEADS * HG_DV, HG_HEADS * HG_DV,
                 2 * DF_HEADS * DF_DH, 2 * DF_HEADS * DF_DH, 2 * DF_HEADS * DF_DH,
                 RW_COLS, BR_W, GATE_RANK]
    for l in range(DEPTH):
        proj = x @ w_in[l]
        hq, hf, hi, hog, dq, dk, dv, rw, xq, gdown = _split(proj, in_widths)
        o_hg = _hgrn2(hq, hf, hi, hog, lb_all[l], hg_norm_g[l])
        lam_init = 0.8 - 0.6 * math.exp(-0.3 * l)
        lam = (jnp.exp(jnp.sum(df_lam_q1[l] * df_lam_k1[l]).astype(f32))
               - jnp.exp(jnp.sum(df_lam_q2[l] * df_lam_k2[l]).astype(f32)) + lam_init)
        o_df = _diff_attn(dq, dk, dv, lam, lam_init, df_subln_g[l])
        o_rw = _rwkv7(rw, rw_shift_mu[l], rw_w0[l], rw_w2[l], rw_a0[l], rw_a2[l], rw_g2[l],
                      rw_k_k[l], rw_k_a[l], rw_r_k[l], rw_lnx_g[l], rw_lnx_b[l])
        mem_k, mem_v = _split(memn @ w_mem_kv[l], [BR_W, BR_W])
        o_xa = _mem_attn(xq, mem_k, mem_v)
        gates = jax.nn.sigmoid((gdown @ w_gate_up[l] + b_gate[l]).astype(f32))
        merged = 0.0
        for n, o_b in enumerate((o_hg, o_df, o_rw, o_xa)):
            merged = merged + gates[..., n * D_MODEL:(n + 1) * D_MODEL] * (o_b @ w_br[l, n])
        y = merged.astype(x.dtype) @ w_o[l]
        x = _layer_norm(DEEPNORM_ALPHA * x + y.astype(x.dtype), ln1_g[l], ln1_b[l])
        y = _moe(x, router_w[l], router_b[l], exp_w1[l], exp_b1[l], exp_w2[l], exp_b2[l])
        x = _layer_norm(DEEPNORM_ALPHA * x + y.astype(x.dtype), ln2_g[l], ln2_b[l])
    return x
```

```python
import functools
import math

import numpy as np
import jax
import jax.numpy as jnp
from jax import lax
from jax.experimental import pallas as pl
from jax.experimental.pallas import tpu as pltpu

F32 = jnp.float32
BF16 = jnp.bfloat16

D_MODEL = 4096
DEPTH = 4
BR_W = D_MODEL // 4
MEM_LEN = 256

HG_DK = 128
HG_HEADS = BR_W // HG_DK
HG_DV = BR_W // HG_HEADS
HG_CHUNK = 64

DF_DH = 64
DF_HEADS = BR_W // (2 * DF_DH)

RW_N = 64
RW_HEADS = BR_W // RW_N
RW_DECAY_LORA = 64
RW_AAA_LORA = 64
RW_GATE_LORA = 160
RW_LORA_COLS = RW_DECAY_LORA + RW_AAA_LORA + RW_GATE_LORA
RW_LORA_PAD = 512
RW_COLS = 3 * BR_W + RW_LORA_COLS
RW_LNX_EPS = 64e-5
RW_CHUNK = 64

XA_HEADS = 4
XA_DH = BR_W // XA_HEADS
GATE_RANK = 256

N_EXPERTS = 32
TOP_K = 4
EXPERT_FF = 256
SWIGLU_LIMIT = 7.0
SWIGLU_ALPHA = 1.702

DEEPNORM_ALPHA = (2 * DEPTH) ** 0.25
LN_EPS = 1e-5

COL_RW = 4 * BR_W + 3 * BR_W
COL_LORA = COL_RW + 3 * BR_W
COL_XQ = COL_RW + RW_COLS
COL_GDOWN = COL_XQ + BR_W
IN_COLS = COL_GDOWN + GATE_RANK
MAIN_COLS = COL_LORA
TAIL_COLS = BR_W + RW_LORA_PAD + GATE_RANK

VMEM_LIMIT = 56 * 1024 * 1024


def _cparams(*sem):
    return pltpu.CompilerParams(dimension_semantics=sem, vmem_limit_bytes=VMEM_LIMIT)


def _dot(a, b):
    return jnp.dot(a, b, preferred_element_type=F32)


def _dot_nt(a, b):
    return lax.dot_general(a, b, (((1,), (1,)), ((), ())), preferred_element_type=F32)


def _dot_tn(a, b):
    return lax.dot_general(a, b, (((0,), (0,)), ((), ())), preferred_element_type=F32)


def _split3(x):
    hi = x.astype(BF16)
    r1 = x - hi.astype(F32)
    mid = r1.astype(BF16)
    lo = (r1 - mid.astype(F32)).astype(BF16)
    return hi, mid, lo


def _dot_sel_left(m01, x):
    hi, mid, lo = _split3(x)
    return _dot(m01, hi) + _dot(m01, mid) + _dot(m01, lo)


def _dot_sel_right(x, m01):
    hi, mid, lo = _split3(x)
    return _dot(hi, m01) + _dot(mid, m01) + _dot(lo, m01)


def _sigmoid(x):
    return 1.0 / (1.0 + jnp.exp(-x))


def _softplus(x):
    return jnp.maximum(x, 0.0) + jnp.log1p(jnp.exp(-jnp.abs(x)))


def _mm_kernel(x_ref, w_ref, o_ref, wbf_ref):
    @pl.when(pl.program_id(1) == 0)
    def _():
        wbf_ref[...] = w_ref[...].astype(BF16)

    o_ref[...] = _dot(x_ref[...].astype(BF16), wbf_ref[...]).astype(o_ref.dtype)


def _matmul(x, w, *, layer=None, ncols=None, tm=512, tn=512, out_dtype=F32, name="matmul"):
    m, k = x.shape
    n = w.shape[-1] if ncols is None else ncols
    tm = min(tm, m)
    assert m % tm == 0 and n % tn == 0
    if layer is None:
        w_spec = pl.BlockSpec((k, tn), lambda j, i: (0, j))
    else:
        w_spec = pl.BlockSpec((None, k, tn), lambda j, i: (layer, 0, j))
    return pl.pallas_call(
        _mm_kernel,
        out_shape=jax.ShapeDtypeStruct((m, n), out_dtype),
        grid=(n // tn, m // tm),
        in_specs=[pl.BlockSpec((tm, k), lambda j, i: (i, 0)), w_spec],
        out_specs=pl.BlockSpec((tm, tn), lambda j, i: (i, j)),
        scratch_shapes=[pltpu.VMEM((k, tn), BF16)],
        compiler_params=_cparams("arbitrary", "arbitrary"),
        name=name,
    )(x, w)


def _ln_kernel(x_ref, g_ref, b_ref, o_ref, obf_ref):
    x = x_ref[...]
    mu = jnp.mean(x, axis=-1, keepdims=True)
    xc = x - mu
    var = jnp.mean(xc * xc, axis=-1, keepdims=True)
    y = xc * lax.rsqrt(var + LN_EPS) * g_ref[...] + b_ref[...]
    o_ref[...] = y
    obf_ref[...] = y.astype(BF16)


def _layer_norm(x, g, b, *, tm=256):
    m, d = x.shape
    tm = min(tm, m)
    row = pl.BlockSpec((tm, d), lambda i: (i, 0))
    vec = pl.BlockSpec((1, d), lambda i: (0, 0))
    return pl.pallas_call(
        _ln_kernel,
        out_shape=(jax.ShapeDtypeStruct((m, d), F32), jax.ShapeDtypeStruct((m, d), BF16)),
        grid=(m // tm,),
        in_specs=[row, vec, vec],
        out_specs=(row, row),
        compiler_params=_cparams("arbitrary"),
        name="layer_norm",
    )(x, g.reshape(1, d), b.reshape(1, d))


def _res_ln_kernel(x_ref, y_ref, g_ref, b_ref, o_ref, obf_ref):
    x = DEEPNORM_ALPHA * x_ref[...] + y_ref[...]
    mu = jnp.mean(x, axis=-1, keepdims=True)
    xc = x - mu
    var = jnp.mean(xc * xc, axis=-1, keepdims=True)
    y = xc * lax.rsqrt(var + LN_EPS) * g_ref[...] + b_ref[...]
    o_ref[...] = y
    obf_ref[...] = y.astype(BF16)


def _residual_layer_norm(x, y, g, b, *, tm=256):
    m, d = x.shape
    tm = min(tm, m)
    row = pl.BlockSpec((tm, d), lambda i: (i, 0))
    vec = pl.BlockSpec((1, d), lambda i: (0, 0))
    return pl.pallas_call(
        _res_ln_kernel,
        out_shape=(jax.ShapeDtypeStruct((m, d), F32), jax.ShapeDtypeStruct((m, d), BF16)),
        grid=(m // tm,),
        in_specs=[row, row, vec, vec],
        out_specs=(row, row),
        compiler_params=_cparams("arbitrary"),
        name="residual_layer_norm",
    )(x, y, g.reshape(1, d), b.reshape(1, d))


_HG_LEVELS = (32, 16, 8, 4, 2, 1)


def _hgrn_constants():
    c = HG_CHUNK
    r = np.arange(c)
    tri = (r[None, :] <= r[:, None]).astype(np.float32)
    mats, masks = [tri], []
    for n in _HG_LEVELS:
        anchor = (r // (2 * n)) * (2 * n) + n - 1
        l_n = (r[None, :] <= anchor[:, None]).astype(np.float32)
        mats.append(tri - l_n)
        same = (r[:, None] // (2 * n)) == (r[None, :] // (2 * n))
        later = ((r[:, None] // n) % 2) == 1
        earlier = ((r[None, :] // n) % 2) == 0
        masks.append((same & later & earlier).astype(np.float32))
    masks.append(np.eye(c, dtype=np.float32))
    return np.concatenate(mats, axis=0), np.stack(masks, axis=0)


def _hgrn_kernel(q_ref, f_ref, i_ref, og_ref, lb_ref, ng_ref, dst_ref, msk_ref, o_ref, st_ref, *, nchunk):
    c = HG_CHUNK

    @pl.when(pl.program_id(1) == 0)
    def _():
        st_ref[...] = jnp.zeros_like(st_ref)

    lb = lb_ref[...]
    log_lb = jnp.log(lb)
    log_1m_lb = jnp.log1p(-lb)
    one_m_lb = 1.0 - lb
    ng = ng_ref[...]
    dst = dst_ref[...]

    def chunk(ci, carry):
        r0 = pl.multiple_of(ci * c, c)
        z = f_ref[pl.ds(r0, c), :]
        qr = q_ref[pl.ds(r0, c), :]
        v = i_ref[pl.ds(r0, c), :]
        og = og_ref[pl.ds(r0, c), :]
        log_sig = jnp.minimum(z, 0.0) - jnp.log1p(jnp.exp(-jnp.abs(z)))
        t1 = log_1m_lb + log_sig
        log_f = jnp.maximum(log_lb, t1) + jnp.log1p(jnp.exp(-jnp.abs(log_lb - t1)))
        k = one_m_lb * _sigmoid(-z)
        q = qr * _sigmoid(qr)
        dall = _dot_sel_left(dst, log_f)
        b = dall[0:c]
        st = st_ref[...]
        o = _dot_nt((q * jnp.exp(b)).astype(BF16), st.astype(BF16))
        att = jnp.where(msk_ref[len(_HG_LEVELS)] > 0.0, _dot_nt(q.astype(BF16), k.astype(BF16)), 0.0)
        for li in range(len(_HG_LEVELS)):
            dn = dall[(li + 1) * c:(li + 2) * c]
            qn = (q * jnp.exp(jnp.minimum(dn, 0.0))).astype(BF16)
            kn = (k * jnp.exp(jnp.minimum(-dn, 0.0))).astype(BF16)
            att = att + jnp.where(msk_ref[li] > 0.0, _dot_nt(qn, kn), 0.0)
        vb = v.astype(BF16)
        o = o + _dot(att.astype(BF16), vb)
        b_last = b[c - 1:c, :]
        kd = (k * jnp.exp(b_last - b)).astype(BF16)
        st_ref[...] = jnp.exp(b_last) * st + _dot_tn(vb, kd)
        rms = lax.rsqrt(jnp.mean(o * o, axis=-1, keepdims=True) + 1e-5)
        out = o * rms * ng * (og * _sigmoid(og))
        o_ref[pl.ds(r0, c), :] = out.astype(o_ref.dtype)
        return carry

    lax.fori_loop(0, nchunk, chunk, 0)


def _hgrn2(proj, lb, norm_g, *, tb=512):
    t = proj.shape[0]
    tb = min(tb, t)
    dst, msk = _hgrn_constants()
    nh = HG_HEADS

    def col(off):
        return pl.BlockSpec((tb, HG_DK), lambda h, i: (i, off + h))

    return pl.pallas_call(
        functools.partial(_hgrn_kernel, nchunk=tb // HG_CHUNK),
        out_shape=jax.ShapeDtypeStruct((t, BR_W), BF16),
        grid=(nh, t // tb),
        in_specs=[col(0), col(nh), col(2 * nh), col(3 * nh),
                  pl.BlockSpec((1, HG_DK), lambda h, i: (0, h)),
                  pl.BlockSpec((1, HG_DV), lambda h, i: (0, 0)),
                  pl.BlockSpec(dst.shape, lambda h, i: (0, 0)),
                  pl.BlockSpec(msk.shape, lambda h, i: (0, 0, 0))],
        out_specs=pl.BlockSpec((tb, HG_DV), lambda h, i: (i, h)),
        scratch_shapes=[pltpu.VMEM((HG_DV, HG_DK), F32)],
        compiler_params=_cparams("arbitrary", "arbitrary"),
        name="hgrn2",
    )(proj, proj, proj, proj, lb.reshape(1, BR_W), norm_g.reshape(1, HG_DV),
      jnp.asarray(dst, BF16), jnp.asarray(msk, F32))


def _dattn_kernel(qi_tab, ki_tab, q_ref, k_ref, v_ref, lam_ref, g_ref, o_ref, m_sc, l_sc, acc_sc,
                  *, tq, out_scale):
    s_id = pl.program_id(1)
    qi = qi_tab[s_id]
    ki = ki_tab[s_id]

    @pl.when(ki == 0)
    def _():
        m_sc[...] = jnp.full_like(m_sc, -jnp.inf)
        l_sc[...] = jnp.zeros_like(l_sc)
        acc_sc[...] = jnp.zeros_like(acc_sc)

    q = q_ref[...] * (DF_DH ** -0.5)
    lane = lax.broadcasted_iota(jnp.int32, q.shape, 1)
    qq = jnp.concatenate([jnp.where(lane < DF_DH, q, 0.0), jnp.where(lane >= DF_DH, q, 0.0)], axis=0)
    s = _dot_nt(qq.astype(BF16), k_ref[...].astype(BF16))
    row = lax.broadcasted_iota(jnp.int32, s.shape, 0)
    row = jnp.where(row >= tq, row - tq, row) + qi * tq
    colp = lax.broadcasted_iota(jnp.int32, s.shape, 1) + ki * tq
    s = jnp.where(colp <= row, s, -jnp.inf)
    m_prev = m_sc[...]
    m_new = jnp.maximum(m_prev, jnp.max(s, axis=1, keepdims=True))
    alpha = jnp.exp(m_prev - m_new)
    p = jnp.exp(s - m_new)
    l_sc[...] = alpha * l_sc[...] + jnp.sum(p, axis=1, keepdims=True)
    acc_sc[...] = alpha * acc_sc[...] + _dot(p.astype(BF16), v_ref[...].astype(BF16))
    m_sc[...] = m_new

    @pl.when(ki == qi)
    def _():
        a = acc_sc[...] / l_sc[...]
        o = a[0:tq] - lam_ref[...] * a[tq:2 * tq]
        rms = lax.rsqrt(jnp.mean(o * o, axis=-1, keepdims=True) + 1e-5)
        o_ref[...] = (o * rms * g_ref[...] * out_scale).astype(o_ref.dtype)


def _diff_attn(proj, lam, lam_init, subln_g, *, tq=512):
    t = proj.shape[0]
    tq = min(tq, t)
    nq = t // tq
    pairs = [(a, b) for a in range(nq) for b in range(a + 1)]
    qi_tab = jnp.asarray([p[0] for p in pairs], jnp.int32)
    ki_tab = jnp.asarray([p[1] for p in pairs], jnp.int32)
    w = 2 * DF_DH
    cq, ck, cv = 4 * BR_W // w, 5 * BR_W // w, 6 * BR_W // w
    grid_spec = pltpu.PrefetchScalarGridSpec(
        num_scalar_prefetch=2,
        grid=(DF_HEADS, len(pairs)),
        in_specs=[pl.BlockSpec((tq, w), lambda h, s, qt, kt: (qt[s], cq + h)),
                  pl.BlockSpec((tq, w), lambda h, s, qt, kt: (kt[s], ck + h)),
                  pl.BlockSpec((tq, w), lambda h, s, qt, kt: (kt[s], cv + h)),
                  pl.BlockSpec((1, w), lambda h, s, qt, kt: (0, 0)),
                  pl.BlockSpec((1, w), lambda h, s, qt, kt: (0, 0))],
        out_specs=pl.BlockSpec((tq, w), lambda h, s, qt, kt: (qt[s], h)),
        scratch_shapes=[pltpu.VMEM((2 * tq, 1), F32), pltpu.VMEM((2 * tq, 1), F32),
                        pltpu.VMEM((2 * tq, w), F32)],
    )
    return pl.pallas_call(
        functools.partial(_dattn_kernel, tq=tq, out_scale=1.0 - lam_init),
        out_shape=jax.ShapeDtypeStruct((t, BR_W), BF16),
        grid_spec=grid_spec,
        compiler_params=_cparams("arbitrary", "arbitrary"),
        name="diff_attn",
    )(qi_tab, ki_tab, proj, proj, proj, jnp.full((1, w), lam, F32), subln_g.reshape(1, w))


def _rw_prep_kernel(r_ref, k_ref, v_ref, lo_ref, rp_ref, kp_ref, vp_ref, lop_ref,
                    mur_ref, muk_ref, muv_ref, mulo_ref, w0_ref, a0_ref, kkw_ref, ka_ref, rk_ref,
                    w2_ref, a2_ref, g2_ref, bd_ref,
                    r_out, lw_out, k_out, v_out, kk_out, b_out, g_out, bonus_out):
    first = pl.program_id(0) == 0

    def shifted(x_ref, xp_ref, mu_ref):
        x = x_ref[...]
        row = lax.broadcasted_iota(jnp.int32, x.shape, 0)
        last_prev = jnp.where(first, 0.0, xp_ref[7:8, :])
        prev = jnp.where(row == 0, last_prev, pltpu.roll(x, 1, 0))
        return x + (prev - x) * mu_ref[...]

    r = shifted(r_ref, rp_ref, mur_ref)
    k = shifted(k_ref, kp_ref, muk_ref)
    v = shifted(v_ref, vp_ref, muv_ref)
    lo = shifted(lo_ref, lop_ref, mulo_ref)
    wd = _dot(jnp.tanh(lo).astype(BF16), w2_ref[...])
    ad = _dot(lo.astype(BF16), a2_ref[...])
    g = _dot(_sigmoid(lo).astype(BF16), g2_ref[...])
    w_log = -_softplus(-(w0_ref[...] + wd)) - 0.5
    lw_out[...] = -jnp.exp(w_log)
    a = _sigmoid(a0_ref[...] + ad)
    bd = bd_ref[...]
    kk = k * kkw_ref[...]
    ss_hi, ss_mid, _ = _split3(kk * kk)
    ss = _dot(ss_hi, bd) + _dot(ss_mid, bd)
    kk = kk / jnp.maximum(jnp.sqrt(ss), 1e-12)
    kmod = k * (1.0 + (a - 1.0) * ka_ref[...])
    rkr = r * kmod * rk_ref[...]
    rk_hi, rk_mid, _ = _split3(rkr)
    bonus_out[...] = (_dot(rk_hi, bd) + _dot(rk_mid, bd)) * v
    r_out[...] = r
    k_out[...] = kmod
    v_out[...] = v
    kk_out[...] = kk
    b_out[...] = kk * a
    g_out[...] = g


def _rw_prep(proj, tail, shift_mu, w0, w2, a0, a2, g2, k_k, k_a, r_k, *, tb=256):
    t = proj.shape[0]
    tb = min(tb, t)
    c = BR_W
    nb8 = tb // 8

    def cur(ncol, blk):
        return pl.BlockSpec((tb, ncol), lambda i: (i, blk))

    def prv(ncol, blk):
        return pl.BlockSpec((8, ncol), lambda i: (jnp.maximum(i * nb8 - 1, 0), blk))

    vec = pl.BlockSpec((1, c), lambda i: (0, 0))
    veclo = pl.BlockSpec((1, RW_LORA_PAD), lambda i: (0, 0))
    wlo = pl.BlockSpec((RW_LORA_PAD, c), lambda i: (0, 0))
    out = pl.BlockSpec((tb, c), lambda i: (i, 0))

    def pad_rows(wm, off):
        return jnp.zeros((RW_LORA_PAD, c), F32).at[off:off + wm.shape[0]].set(wm).astype(BF16)

    mu_lo = jnp.zeros((RW_LORA_PAD,), F32).at[:RW_LORA_COLS].set(shift_mu[3 * c:])
    hid = np.arange(c) // RW_N
    bd = jnp.asarray((hid[:, None] == hid[None, :]).astype(np.float32), BF16)
    row = lambda a: a.reshape(1, -1)
    cb = COL_RW // c
    return pl.pallas_call(
        _rw_prep_kernel,
        out_shape=tuple(jax.ShapeDtypeStruct((t, c), F32) for _ in range(8)),
        grid=(t // tb,),
        in_specs=[cur(c, cb), cur(c, cb + 1), cur(c, cb + 2), cur(RW_LORA_PAD, BR_W // RW_LORA_PAD),
                  prv(c, cb), prv(c, cb + 1), prv(c, cb + 2), prv(RW_LORA_PAD, BR_W // RW_LORA_PAD),
                  vec, vec, vec, veclo, vec, vec, vec, vec, vec, wlo, wlo, wlo,
                  pl.BlockSpec((c, c), lambda i: (0, 0))],
        out_specs=tuple(out for _ in range(8)),
        compiler_params=_cparams("arbitrary"),
        name="rwkv7_prep",
    )(proj, proj, proj, tail, proj, proj, proj, tail,
      row(shift_mu[:c]), row(shift_mu[c:2 * c]), row(shift_mu[2 * c:3 * c]), row(mu_lo),
      row(w0), row(a0), row(k_k), row(k_a), row(r_k),
      pad_rows(w2, 0), pad_rows(a2, RW_DECAY_LORA), pad_rows(g2, RW_DECAY_LORA + RW_AAA_LORA), bd)


def _rw_masks():
    c = RW_CHUNK
    r = np.arange(c)
    tri = r[None, :] <= r[:, None]
    strict = r[None, :] < r[:, None]
    same16 = (r[:, None] // 16) == (r[None, :] // 16)
    same32 = (r[:, None] // 32) == (r[None, :] // 32)
    m = np.stack([tri, strict, same16, same32 & ~same16, ~same32, np.eye(c, dtype=bool)], axis=0)
    return m.astype(np.float32)


def _rw_rec_kernel(r_ref, lw_ref, k_ref, v_ref, kk_ref, b_ref, msk_ref, o_ref, st_ref):
    c = RW_CHUNK

    @pl.when(pl.program_id(0) == 0)
    def _():
        st_ref[...] = jnp.zeros_like(st_ref)

    tri = msk_ref[0]
    lower = tri > 0.0
    strict = msk_ref[1] > 0.0
    in16 = msk_ref[2] > 0.0
    in32 = msk_ref[3] > 0.0
    off32 = msk_ref[4] > 0.0
    eye = msk_ref[5]
    tri_bf = tri.astype(BF16)
    bf = lambda x: x.astype(BF16)

    def head(h, carry):
        lw = lw_ref[h]
        r = r_ref[h]
        k = k_ref[h]
        v = v_ref[h]
        a = -kk_ref[h]
        b = b_ref[h]
        cs = _dot_sel_left(tri_bf, lw)
        e_inc = jnp.exp(cs)
        e_exc = jnp.exp(cs - lw)
        e_neg = jnp.exp(-cs)
        at = bf(a * e_exc)
        rt = bf(r * e_inc)
        bt = bf(b * e_neg)
        kt = bf(k * e_neg)
        vb = bf(v)
        ar = jnp.concatenate([at, rt], axis=0)
        pb = _dot_nt(ar, bt)
        pk = _dot_nt(ar, kt)
        a_ab = jnp.where(strict, pb[0:c], 0.0)
        a_rb = jnp.where(lower, pb[c:2 * c], 0.0)
        a_ak = jnp.where(strict, pk[0:c], 0.0)
        a_rk = jnp.where(lower, pk[c:2 * c], 0.0)
        ld = jnp.where(in16, a_ab, 0.0)
        t1 = eye + ld
        l2 = _dot(bf(ld), bf(ld))
        t2 = t1 + _dot(bf(t1), bf(l2))
        l4 = _dot(bf(l2), bf(l2))
        t3 = t2 + _dot(bf(t2), bf(l4))
        l8 = _dot(bf(l4), bf(l4))
        td = t3 + _dot(bf(t3), bf(l8))
        lo1 = jnp.where(in32, a_ab, 0.0)
        t32 = td + _dot(bf(_dot(bf(td), bf(lo1))), bf(td))
        lo2 = jnp.where(off32, a_ab, 0.0)
        tinv = t32 + _dot(bf(_dot(bf(t32), bf(lo2))), bf(t32))
        st = st_ref[h]
        stb = bf(st)
        arh = _dot_nt(ar, stb)
        u = _dot(bf(tinv), bf(arh[0:c] + _dot(bf(a_ak), vb)))
        o = arh[c:2 * c] + _dot(bf(a_rb), bf(u)) + _dot(bf(a_rk), vb)
        c_last = cs[c - 1:c, :]
        e_tail = jnp.exp(c_last - cs)
        uv = jnp.concatenate([bf(u), vb], axis=0)
        bk = jnp.concatenate([bf(b * e_tail), bf(k * e_tail)], axis=0)
        st_ref[h] = jnp.exp(c_last) * st + _dot_tn(uv, bk)
        mu = jnp.mean(o, axis=-1, keepdims=True)
        oc = o - mu
        var = jnp.mean(oc * oc, axis=-1, keepdims=True)
        o_ref[h] = oc * lax.rsqrt(var + RW_LNX_EPS)
        return carry

    lax.fori_loop(0, RW_HEADS, head, 0, unroll=2)


def _rw_recurrence(r, lw, k, v, kk, b):
    nh, t, n = r.shape
    c = RW_CHUNK
    msk = _rw_masks()
    blk = pl.BlockSpec((nh, c, n), lambda i: (0, i, 0))
    return pl.pallas_call(
        _rw_rec_kernel,
        out_shape=jax.ShapeDtypeStruct((nh, t, n), F32),
        grid=(t // c,),
        in_specs=[blk] * 6 + [pl.BlockSpec(msk.shape, lambda i: (0, 0, 0))],
        out_specs=blk,
        scratch_shapes=[pltpu.VMEM((nh, n, n), F32)],
        compiler_params=_cparams("arbitrary"),
        name="rwkv7_recurrence",
    )(r, lw, k, v, kk, b, jnp.asarray(msk, F32))


def _rw_post_kernel(o_ref, bonus_ref, g_ref, lg_ref, lb_ref, out_ref):
    out_ref[...] = ((o_ref[...] * lg_ref[...] + lb_ref[...] + bonus_ref[...]) * g_ref[...]).astype(out_ref.dtype)


def _rw_post(o, bonus, g, lnx_g, lnx_b, *, tb=512):
    t, c = o.shape
    tb = min(tb, t)
    blk = pl.BlockSpec((tb, c), lambda i: (i, 0))
    vec = pl.BlockSpec((1, c), lambda i: (0, 0))
    return pl.pallas_call(
        _rw_post_kernel,
        out_shape=jax.ShapeDtypeStruct((t, c), BF16),
        grid=(t // tb,),
        in_specs=[blk, blk, blk, vec, vec],
        out_specs=blk,
        compiler_params=_cparams("arbitrary"),
        name="rwkv7_post",
    )(o, bonus, g, lnx_g.reshape(1, c), lnx_b.reshape(1, c))


def _rwkv7(proj, tail, shift_mu, w0, w2, a0, a2, g2, k_k, k_a, r_k, lnx_g, lnx_b):
    t = proj.shape[0]
    r, lw, k, v, kk, b, g, bonus = _rw_prep(proj, tail, shift_mu, w0, w2, a0, a2, g2, k_k, k_a,
                                            r_k.reshape(-1))
    heads = lambda x: x.reshape(t, RW_HEADS, RW_N).transpose(1, 0, 2)
    o = _rw_recurrence(heads(r), heads(lw), heads(k), heads(v), heads(kk), heads(b))
    o = o.transpose(1, 0, 2).reshape(t, BR_W)
    return _rw_post(o, bonus, g, lnx_g, lnx_b)


def _xattn_kernel(q_ref, kv_ref, o_ref):
    for h in range(XA_HEADS):
        q = q_ref[:, h * XA_DH:(h + 1) * XA_DH] * (XA_DH ** -0.5)
        k = kv_ref[:, h * XA_DH:(h + 1) * XA_DH]
        v = kv_ref[:, BR_W + h * XA_DH:BR_W + (h + 1) * XA_DH]
        s = _dot_nt(q.astype(BF16), k.astype(BF16))
        p = jnp.exp(s - jnp.max(s, axis=-1, keepdims=True))
        o = _dot(p.astype(BF16), v.astype(BF16)) / jnp.sum(p, axis=-1, keepdims=True)
        o_ref[:, h * XA_DH:(h + 1) * XA_DH] = o.astype(o_ref.dtype)


def _mem_attn(tail, mem_kv, *, tq=512):
    t = tail.shape[0]
    tq = min(tq, t)
    return pl.pallas_call(
        _xattn_kernel,
        out_shape=jax.ShapeDtypeStruct((t, BR_W), BF16),
        grid=(t // tq,),
        in_specs=[pl.BlockSpec((tq, BR_W), lambda i: (i, 0)),
                  pl.BlockSpec(mem_kv.shape, lambda i: (0, 0))],
        out_specs=pl.BlockSpec((tq, BR_W), lambda i: (i, 0)),
        compiler_params=_cparams("arbitrary"),
        name="mem_attn",
    )(tail, mem_kv)


def _merge_kernel(o0_ref, o1_ref, o2_ref, o3_ref, gd_ref, wbr_ref, wg0_ref, wg1_ref, wg2_ref, wg3_ref,
                  bg0_ref, bg1_ref, bg2_ref, bg3_ref, out_ref, wbr_bf, wg_bf):
    wg_refs = (wg0_ref, wg1_ref, wg2_ref, wg3_ref)

    @pl.when(pl.program_id(1) == 0)
    def _():
        wbr_bf[...] = wbr_ref[...].astype(BF16)
        for n in range(4):
            wg_bf[n] = wg_refs[n][...].astype(BF16)

    gd = gd_ref[...].astype(BF16)
    acc = None
    for n, (o_ref, bg_ref) in enumerate(zip((o0_ref, o1_ref, o2_ref, o3_ref),
                                            (bg0_ref, bg1_ref, bg2_ref, bg3_ref))):
        gate = _sigmoid(_dot(gd, wg_bf[n]) + bg_ref[...])
        term = gate * _dot(o_ref[...], wbr_bf[n])
        acc = term if acc is None else acc + term
    out_ref[...] = acc.astype(out_ref.dtype)


def _merge(branches, tail, w_br, w_gate_up, b_gate, layer, *, tm=512, tn=512):
    t = tail.shape[0]
    tm = min(tm, t)
    d = D_MODEL
    nj = d // tn
    ob = pl.BlockSpec((tm, BR_W), lambda j, i: (i, 0))
    gd_blk = (BR_W + RW_LORA_PAD) // GATE_RANK
    wg = [pl.BlockSpec((None, GATE_RANK, tn), lambda j, i, n=n: (layer, 0, n * nj + j)) for n in range(4)]
    bg = [pl.BlockSpec((None, 1, tn), lambda j, i, n=n: (layer, 0, n * nj + j)) for n in range(4)]
    bgate3 = b_gate.reshape(DEPTH, 1, 4 * d)
    return pl.pallas_call(
        _merge_kernel,
        out_shape=jax.ShapeDtypeStruct((t, d), BF16),
        grid=(nj, t // tm),
        in_specs=[ob, ob, ob, ob,
                  pl.BlockSpec((tm, GATE_RANK), lambda j, i: (i, gd_blk)),
                  pl.BlockSpec((None, 4, BR_W, tn), lambda j, i: (layer, 0, 0, j))] + wg + bg,
        out_specs=pl.BlockSpec((tm, tn), lambda j, i: (i, j)),
        scratch_shapes=[pltpu.VMEM((4, BR_W, tn), BF16), pltpu.VMEM((4, GATE_RANK, tn), BF16)],
        compiler_params=_cparams("arbitrary", "arbitrary"),
        name="gated_merge",
    )(*branches, tail, w_br, w_gate_up, w_gate_up, w_gate_up, w_gate_up, bgate3, bgate3, bgate3, bgate3)


def _router_kernel(x_ref, w_ref, b_ref, g_ref):
    xh, xm, _ = _split3(x_ref[...])
    wh, wm, _ = _split3(w_ref[...])
    logits = _dot(xh, wh) + _dot(xh, wm) + _dot(xm, wh) + b_ref[...]
    lane = lax.broadcasted_iota(jnp.int32, logits.shape, 1)
    work = jnp.where(lane < N_EXPERTS, logits, -jnp.inf)
    top = jnp.max(work, axis=-1, keepdims=True)
    gates = jnp.zeros_like(logits)
    denom = jnp.zeros_like(top)
    for _ in range(TOP_K):
        cur = jnp.max(work, axis=-1, keepdims=True)
        idx = jnp.min(jnp.where(work == cur, lane, 2 * 128), axis=-1, keepdims=True)
        sel = lane == idx
        e = jnp.exp(cur - top)
        gates = jnp.where(sel, e, gates)
        denom = denom + e
        work = jnp.where(sel, -jnp.inf, work)
    g_ref[...] = gates / denom


def _router(x, router_w, router_b, *, tm=512):
    t, d = x.shape
    tm = min(tm, t)
    wpad = jnp.zeros((d, 128), F32).at[:, :N_EXPERTS].set(router_w)
    bpad = jnp.zeros((1, 128), F32).at[0, :N_EXPERTS].set(router_b)
    return pl.pallas_call(
        _router_kernel,
        out_shape=jax.ShapeDtypeStruct((t, 128), F32),
        grid=(t // tm,),
        in_specs=[pl.BlockSpec((tm, d), lambda i: (i, 0)),
                  pl.BlockSpec((d, 128), lambda i: (0, 0)),
                  pl.BlockSpec((1, 128), lambda i: (0, 0))],
        out_specs=pl.BlockSpec((tm, 128), lambda i: (i, 0)),
        compiler_params=_cparams("arbitrary"),
        name="moe_router",
    )(x, wpad, bpad)


def _w1_prep_kernel(w_ref, p_ref, o_ref):
    o_ref[...] = _dot(w_ref[...].astype(BF16), p_ref[...]).astype(BF16)


def _w1_prep(exp_w1, layer, *, tk=1024):
    e, d, f2 = exp_w1.shape[1:]
    perm = np.concatenate([np.arange(0, f2, 2), np.arange(1, f2, 2)])
    pmat = np.zeros((f2, f2), np.float32)
    pmat[perm, np.arange(f2)] = 1.0
    return pl.pallas_call(
        _w1_prep_kernel,
        out_shape=jax.ShapeDtypeStruct((e, d, f2), BF16),
        grid=(e, d // tk),
        in_specs=[pl.BlockSpec((None, None, tk, f2), lambda a, b: (layer, a, b, 0)),
                  pl.BlockSpec((f2, f2), lambda a, b: (0, 0))],
        out_specs=pl.BlockSpec((None, tk, f2), lambda a, b: (a, b, 0)),
        compiler_params=_cparams("arbitrary", "arbitrary"),
        name="moe_w1_prep",
    )(exp_w1, jnp.asarray(pmat, BF16))


def _moe_kernel(x_ref, g_ref, w1_ref, b1_ref, w2_ref, b2_ref, o_ref):
    e = pl.program_id(1)
    gates = g_ref[...]

    @pl.when(e == 0)
    def _():
        gh, gm, _ = _split3(gates)
        bh, bm, _ = _split3(b2_ref[...])
        o_ref[...] = _dot(gh, bh) + _dot(gh, bm) + _dot(gm, bh)

    hid = _dot(x_ref[...], w1_ref[...]) + b1_ref[...]
    x_glu = jnp.minimum(hid[:, :EXPERT_FF], SWIGLU_LIMIT)
    x_lin = jnp.clip(hid[:, EXPERT_FF:], -SWIGLU_LIMIT, SWIGLU_LIMIT)
    act = x_glu * _sigmoid(SWIGLU_ALPHA * x_glu) * (x_lin + 1.0)
    lane = lax.broadcasted_iota(jnp.int32, gates.shape, 1)
    ge = jnp.sum(jnp.where(lane == e, gates, 0.0), axis=-1, keepdims=True)
    o_ref[...] += _dot((act * ge).astype(BF16), w2_ref[...].astype(BF16))


def _moe(x_bf, gates, w1p, b1p, exp_w2, exp_b2, layer, *, tm=512):
    t, d = x_bf.shape
    tm = min(tm, t)
    f2 = 2 * EXPERT_FF
    b2pad = jnp.zeros((128, d), F32).at[:N_EXPERTS].set(exp_b2[layer])
    return pl.pallas_call(
        _moe_kernel,
        out_shape=jax.ShapeDtypeStruct((t, d), F32),
        grid=(t // tm, N_EXPERTS),
        in_specs=[pl.BlockSpec((tm, d), lambda i, e: (i, 0)),
                  pl.BlockSpec((tm, 128), lambda i, e: (i, 0)),
                  pl.BlockSpec((None, d, f2), lambda i, e: (e, 0, 0)),
                  pl.BlockSpec((None, 1, f2), lambda i, e: (e, 0, 0)),
                  pl.BlockSpec((None, None, EXPERT_FF, d), lambda i, e: (layer, e, 0, 0)),
                  pl.BlockSpec((128, d), lambda i, e: (0, 0))],
        out_specs=pl.BlockSpec((tm, d), lambda i, e: (i, 0)),
        compiler_params=_cparams("arbitrary", "arbitrary"),
        name="moe_experts",
    )(x_bf, gates, w1p, b1p, exp_w2, b2pad)


def _tail_weight(w_in_l):
    lora = jnp.pad(w_in_l[:, COL_LORA:COL_XQ], ((0, 0), (0, RW_LORA_PAD - RW_LORA_COLS)))
    return jnp.concatenate([w_in_l[:, COL_XQ:COL_GDOWN], lora, w_in_l[:, COL_GDOWN:]], axis=1)


def kernel(x, mem, mem_ln_g, mem_ln_b, w_in, hg_lb_raw, hg_norm_g, df_lam_q1, df_lam_k1, df_lam_q2, df_lam_k2, df_subln_g, rw_shift_mu, rw_w0, rw_w2, rw_a0, rw_a2, rw_g2, rw_k_k, rw_k_a, rw_r_k, rw_lnx_g, rw_lnx_b, w_mem_kv, w_br, w_gate_up, b_gate, w_o, ln1_g, ln1_b, router_w, router_b, exp_w1, exp_b1, exp_w2, exp_b2, ln2_g, ln2_b):
    bsz, t, d = x.shape
    assert bsz == 1
    xf = x.reshape(t, d)
    xb = xf.astype(BF16)
    _, memn = _layer_norm(mem.reshape(-1, d), mem_ln_g, mem_ln_b)
    lb_cum = jnp.cumsum(jax.nn.softmax(hg_lb_raw.astype(F32), axis=0), axis=0)
    lb_all = lb_cum - lb_cum[:1]
    f2 = 2 * EXPERT_FF
    perm = np.concatenate([np.arange(0, f2, 2), np.arange(1, f2, 2)])
    for l in range(DEPTH):
        proj = _matmul(xb, w_in, layer=l, ncols=MAIN_COLS, name="in_proj_main")
        tail = _matmul(xb, _tail_weight(w_in[l]), tn=256, name="in_proj_tail")
        o_hg = _hgrn2(proj, lb_all[l], hg_norm_g[l])
        lam_init = 0.8 - 0.6 * math.exp(-0.3 * l)
        lam = (jnp.exp(jnp.sum(df_lam_q1[l] * df_lam_k1[l]).astype(F32))
               - jnp.exp(jnp.sum(df_lam_q2[l] * df_lam_k2[l]).astype(F32)) + lam_init)
        o_df = _diff_attn(proj, lam, lam_init, df_subln_g[l])
        o_rw = _rwkv7(proj, tail, rw_shift_mu[l], rw_w0[l], rw_w2[l], rw_a0[l], rw_a2[l], rw_g2[l],
                      rw_k_k[l], rw_k_a[l], rw_r_k[l], rw_lnx_g[l], rw_lnx_b[l])
        mem_kv = _matmul(memn, w_mem_kv, layer=l, name="mem_kv")
        o_xa = _mem_attn(tail, mem_kv)
        merged = _merge((o_hg, o_df, o_rw, o_xa), tail, w_br, w_gate_up, b_gate, l)
        y = _matmul(merged, w_o, layer=l, name="out_proj")
        xf, xb = _residual_layer_norm(xf, y, ln1_g[l], ln1_b[l])
        gates = _router(xf, router_w[l], router_b[l])
        w1p = _w1_prep(exp_w1, l)
        b1p = exp_b1[l][:, perm].reshape(N_EXPERTS, 1, f2)
        y = _moe(xb, gates, w1p, b1p, exp_w2, exp_b2, l)
        xf, xb = _residual_layer_norm(xf, y, ln2_g[l], ln2_b[l])
    return xf.reshape(bsz, t, d)
```

```python
import functools
import math

import numpy as np
import jax
import jax.numpy as jnp
from jax import lax
from jax.experimental import pallas as pl
from jax.experimental.pallas import tpu as pltpu

F32 = jnp.float32
BF16 = jnp.bfloat16

D_MODEL = 4096
DEPTH = 4
BR_W = D_MODEL // 4
MEM_LEN = 256

HG_DK = 128
HG_HEADS = BR_W // HG_DK
HG_DV = BR_W // HG_HEADS
HG_CHUNK = 64

DF_DH = 64
DF_HEADS = BR_W // (2 * DF_DH)

RW_N = 64
RW_HEADS = BR_W // RW_N
RW_DECAY_LORA = 64
RW_AAA_LORA = 64
RW_GATE_LORA = 160
RW_LORA_COLS = RW_DECAY_LORA + RW_AAA_LORA + RW_GATE_LORA
RW_LORA_PAD = 512
RW_COLS = 3 * BR_W + RW_LORA_COLS
RW_LNX_EPS = 64e-5
RW_CHUNK = 64

XA_HEADS = 4
XA_DH = BR_W // XA_HEADS
GATE_RANK = 256

N_EXPERTS = 32
TOP_K = 4
EXPERT_FF = 256
SWIGLU_LIMIT = 7.0
SWIGLU_ALPHA = 1.702

DEEPNORM_ALPHA = (2 * DEPTH) ** 0.25
LN_EPS = 1e-5

COL_RW = 4 * BR_W + 3 * BR_W
COL_LORA = COL_RW + 3 * BR_W
COL_XQ = COL_RW + RW_COLS
COL_GDOWN = COL_XQ + BR_W
IN_COLS = COL_GDOWN + GATE_RANK
MAIN_COLS = COL_LORA
TAIL_COLS = BR_W + RW_LORA_PAD + GATE_RANK

VMEM_LIMIT = 56 * 1024 * 1024


def _cparams(*sem):
    return pltpu.CompilerParams(dimension_semantics=sem, vmem_limit_bytes=VMEM_LIMIT)


def _dot(a, b):
    return jnp.dot(a, b, preferred_element_type=F32)


def _dot_nt(a, b):
    return lax.dot_general(a, b, (((1,), (1,)), ((), ())), preferred_element_type=F32)


def _dot_tn(a, b):
    return lax.dot_general(a, b, (((0,), (0,)), ((), ())), preferred_element_type=F32)


def _bdot(a, b):
    return lax.dot_general(a, b, (((2,), (1,)), ((0,), (0,))), preferred_element_type=F32)


def _bdot_nt(a, b):
    return lax.dot_general(a, b, (((2,), (2,)), ((0,), (0,))), preferred_element_type=F32)


def _bdot_tn(a, b):
    return lax.dot_general(a, b, (((1,), (1,)), ((0,), (0,))), preferred_element_type=F32)


def _split3(x):
    hi = x.astype(BF16)
    r1 = x - hi.astype(F32)
    mid = r1.astype(BF16)
    lo = (r1 - mid.astype(F32)).astype(BF16)
    return hi, mid, lo


def _dot_sel_left(m01, x):
    hi, mid, lo = _split3(x)
    return _dot(m01, hi) + _dot(m01, mid) + _dot(m01, lo)


def _dot_sel_right(x, m01):
    hi, mid, lo = _split3(x)
    return _dot(hi, m01) + _dot(mid, m01) + _dot(lo, m01)


def _sigmoid(x):
    return 1.0 / (1.0 + jnp.exp(-x))


def _softplus(x):
    return jnp.maximum(x, 0.0) + jnp.log1p(jnp.exp(-jnp.abs(x)))


def _mm_kernel(x_ref, w_ref, o_ref, wbf_ref):
    @pl.when(pl.program_id(1) == 0)
    def _():
        wbf_ref[...] = w_ref[...].astype(BF16)

    o_ref[...] = _dot(x_ref[...].astype(BF16), wbf_ref[...]).astype(o_ref.dtype)


def _matmul(x, w, *, layer=None, ncols=None, tm=512, tn=512, out_dtype=F32, name="matmul"):
    m, k = x.shape
    n = w.shape[-1] if ncols is None else ncols
    tm = min(tm, m)
    assert m % tm == 0 and n % tn == 0
    if layer is None:
        w_spec = pl.BlockSpec((k, tn), lambda j, i: (0, j))
    else:
        w_spec = pl.BlockSpec((None, k, tn), lambda j, i: (layer, 0, j))
    return pl.pallas_call(
        _mm_kernel,
        out_shape=jax.ShapeDtypeStruct((m, n), out_dtype),
        grid=(n // tn, m // tm),
        in_specs=[pl.BlockSpec((tm, k), lambda j, i: (i, 0)), w_spec],
        out_specs=pl.BlockSpec((tm, tn), lambda j, i: (i, j)),
        scratch_shapes=[pltpu.VMEM((k, tn), BF16)],
        compiler_params=_cparams("arbitrary", "arbitrary"),
        name=name,
    )(x, w)


def _mm_nt_kernel(x_ref, wt_ref, o_ref, wbf_ref):
    @pl.when(pl.program_id(1) == 0)
    def _():
        wbf_ref[...] = wt_ref[...].astype(BF16)

    o_ref[...] = _dot_nt(x_ref[...].astype(BF16), wbf_ref[...]).astype(o_ref.dtype)


def _matmul_nt(x, wt, *, layer=None, nrows=None, tm=512, tn=512, out_dtype=F32, name="matmul_nt"):
    m, k = x.shape
    n = wt.shape[-2] if nrows is None else nrows
    tm = min(tm, m)
    assert m % tm == 0 and n % tn == 0
    if layer is None:
        w_spec = pl.BlockSpec((tn, k), lambda j, i: (j, 0))
    else:
        w_spec = pl.BlockSpec((None, tn, k), lambda j, i: (layer, j, 0))
    return pl.pallas_call(
        _mm_nt_kernel,
        out_shape=jax.ShapeDtypeStruct((m, n), out_dtype),
        grid=(n // tn, m // tm),
        in_specs=[pl.BlockSpec((tm, k), lambda j, i: (i, 0)), w_spec],
        out_specs=pl.BlockSpec((tm, tn), lambda j, i: (i, j)),
        scratch_shapes=[pltpu.VMEM((tn, k), BF16)],
        compiler_params=_cparams("arbitrary", "arbitrary"),
        name=name,
    )(x, wt)


def _ln_kernel(x_ref, g_ref, b_ref, o_ref, obf_ref):
    x = x_ref[...]
    mu = jnp.mean(x, axis=-1, keepdims=True)
    xc = x - mu
    var = jnp.mean(xc * xc, axis=-1, keepdims=True)
    y = xc * lax.rsqrt(var + LN_EPS) * g_ref[...] + b_ref[...]
    o_ref[...] = y
    obf_ref[...] = y.astype(BF16)


def _layer_norm(x, g, b, *, tm=256):
    m, d = x.shape
    tm = min(tm, m)
    row = pl.BlockSpec((tm, d), lambda i: (i, 0))
    vec = pl.BlockSpec((1, d), lambda i: (0, 0))
    return pl.pallas_call(
        _ln_kernel,
        out_shape=(jax.ShapeDtypeStruct((m, d), F32), jax.ShapeDtypeStruct((m, d), BF16)),
        grid=(m // tm,),
        in_specs=[row, vec, vec],
        out_specs=(row, row),
        compiler_params=_cparams("arbitrary"),
        name="layer_norm",
    )(x, g.reshape(1, d), b.reshape(1, d))


def _res_ln_kernel(x_ref, y_ref, g_ref, b_ref, o_ref, obf_ref):
    x = DEEPNORM_ALPHA * x_ref[...] + y_ref[...]
    mu = jnp.mean(x, axis=-1, keepdims=True)
    xc = x - mu
    var = jnp.mean(xc * xc, axis=-1, keepdims=True)
    y = xc * lax.rsqrt(var + LN_EPS) * g_ref[...] + b_ref[...]
    o_ref[...] = y
    obf_ref[...] = y.astype(BF16)


def _residual_layer_norm(x, y, g, b, *, tm=256):
    m, d = x.shape
    tm = min(tm, m)
    row = pl.BlockSpec((tm, d), lambda i: (i, 0))
    vec = pl.BlockSpec((1, d), lambda i: (0, 0))
    return pl.pallas_call(
        _res_ln_kernel,
        out_shape=(jax.ShapeDtypeStruct((m, d), F32), jax.ShapeDtypeStruct((m, d), BF16)),
        grid=(m // tm,),
        in_specs=[row, row, vec, vec],
        out_specs=(row, row),
        compiler_params=_cparams("arbitrary"),
        name="residual_layer_norm",
    )(x, y, g.reshape(1, d), b.reshape(1, d))


_HG_LEVELS = (32, 16, 8, 4, 2, 1)


def _hgrn_constants():
    c = HG_CHUNK
    r = np.arange(c)
    tri = (r[None, :] <= r[:, None]).astype(np.float32)
    mats, masks = [tri], []
    for n in _HG_LEVELS:
        anchor = (r // (2 * n)) * (2 * n) + n - 1
        l_n = (r[None, :] <= anchor[:, None]).astype(np.float32)
        mats.append(tri - l_n)
        same = (r[:, None] // (2 * n)) == (r[None, :] // (2 * n))
        later = ((r[:, None] // n) % 2) == 1
        earlier = ((r[None, :] // n) % 2) == 0
        masks.append((same & later & earlier).astype(np.float32))
    masks.append(np.eye(c, dtype=np.float32))
    return np.concatenate(mats, axis=0), np.stack(masks, axis=0)


def _hgrn_kernel(q_ref, f_ref, i_ref, og_ref, lb_ref, ng_ref, dst_ref, msk_ref, o_ref, st_ref, *, nchunk):
    c = HG_CHUNK

    nh, dk, dv = HG_HEADS, HG_DK, HG_DV

    @pl.when(pl.program_id(0) == 0)
    def _():
        st_ref[...] = jnp.zeros_like(st_ref)

    lb = lb_ref[...]
    log_lb = jnp.log(lb)
    log_1m_lb = jnp.log1p(-lb)
    one_m_lb = 1.0 - lb
    ng = ng_ref[...]
    dst_b = jnp.broadcast_to(dst_ref[...], (nh,) + dst_ref.shape)
    bf = lambda x: x.astype(BF16)
    heads = lambda x: jnp.stack([x[:, h * dk:(h + 1) * dk] for h in range(nh)], axis=0)

    def chunk(ci, carry):
        r0 = pl.multiple_of(ci * c, c)
        z = f_ref[pl.ds(r0, c), :]
        qr = q_ref[pl.ds(r0, c), :]
        og = og_ref[pl.ds(r0, c), :]
        log_sig = jnp.minimum(z, 0.0) - jnp.log1p(jnp.exp(-jnp.abs(z)))
        t1 = log_1m_lb + log_sig
        log_f = jnp.maximum(log_lb, t1) + jnp.log1p(jnp.exp(-jnp.abs(log_lb - t1)))
        k = heads(one_m_lb * _sigmoid(-z))
        q = heads(qr * _sigmoid(qr))
        vb = bf(heads(i_ref[pl.ds(r0, c), :]))
        lf_hi, lf_mid, lf_lo = _split3(heads(log_f))
        dall = _bdot(dst_b, lf_hi) + _bdot(dst_b, lf_mid) + _bdot(dst_b, lf_lo)
        b = dall[:, 0:c]
        st = st_ref[...]
        o = _bdot_nt(bf(q * jnp.exp(b)), bf(st))
        att = jnp.where(msk_ref[len(_HG_LEVELS)] > 0.0, _bdot_nt(bf(q), bf(k)), 0.0)
        for li in range(len(_HG_LEVELS)):
            dn = dall[:, (li + 1) * c:(li + 2) * c]
            qn = bf(q * jnp.exp(jnp.minimum(dn, 0.0)))
            kn = bf(k * jnp.exp(jnp.minimum(-dn, 0.0)))
            att = att + jnp.where(msk_ref[li] > 0.0, _bdot_nt(qn, kn), 0.0)
        o = o + _bdot(bf(att), vb)
        b_last = b[:, c - 1:c, :]
        kd = bf(k * jnp.exp(b_last - b))
        st_ref[...] = jnp.exp(b_last) * st + _bdot_tn(vb, kd)
        rms = lax.rsqrt(jnp.mean(o * o, axis=-1, keepdims=True) + 1e-5)
        out = o * rms * ng
        gate = og * _sigmoid(og)
        for h in range(nh):
            cols = slice(h * dv, (h + 1) * dv)
            o_ref[pl.ds(r0, c), cols] = (out[h] * gate[:, cols]).astype(o_ref.dtype)
        return carry

    lax.fori_loop(0, nchunk, chunk, 0)


def _hgrn2(proj, lb, norm_g, *, tb=512):
    t = proj.shape[0]
    tb = min(tb, t)
    dst, msk = _hgrn_constants()

    def col(blk):
        return pl.BlockSpec((tb, BR_W), lambda i: (i, blk))

    return pl.pallas_call(
        functools.partial(_hgrn_kernel, nchunk=tb // HG_CHUNK),
        out_shape=jax.ShapeDtypeStruct((t, BR_W), BF16),
        grid=(t // tb,),
        in_specs=[col(0), col(1), col(2), col(3),
                  pl.BlockSpec((1, BR_W), lambda i: (0, 0)),
                  pl.BlockSpec((1, HG_DV), lambda i: (0, 0)),
                  pl.BlockSpec(dst.shape, lambda i: (0, 0)),
                  pl.BlockSpec(msk.shape, lambda i: (0, 0, 0))],
        out_specs=pl.BlockSpec((tb, BR_W), lambda i: (i, 0)),
        scratch_shapes=[pltpu.VMEM((HG_HEADS, HG_DV, HG_DK), F32)],
        compiler_params=_cparams("arbitrary"),
        name="hgrn2",
    )(proj, proj, proj, proj, lb.reshape(1, BR_W), norm_g.reshape(1, HG_DV),
      jnp.asarray(dst, BF16), jnp.asarray(msk, F32))


LOG2E = 1.4426950408889634


def _dattn_kernel(q_ref, k_ref, v_ref, lam_ref, g_ref, o_ref, kbf_sc, vbf_sc, qq_sc, m_sc, l_sc, acc_sc,
                  *, tq, rg, out_scale):
    qi = pl.program_id(1)
    w = 2 * DF_DH
    rep = tq // w

    @pl.when(qi == 0)
    def _():
        kbf_sc[...] = k_ref[...].astype(BF16)
        vbf_sc[...] = v_ref[...].astype(BF16)

    q = q_ref[...] * (DF_DH ** -0.5 * LOG2E)
    lane = lax.broadcasted_iota(jnp.int32, q.shape, 1)
    qq_sc[0:tq, :] = jnp.where(lane < DF_DH, q, 0.0).astype(BF16)
    qq_sc[tq:2 * tq, :] = jnp.where(lane >= DF_DH, q, 0.0).astype(BF16)
    m_sc[...] = jnp.full_like(m_sc, -jnp.inf)
    l_sc[...] = jnp.zeros_like(l_sc)
    acc_sc[...] = jnp.zeros_like(acc_sc)

    def block(kb, masked):
        r0 = pl.multiple_of(kb * tq, tq)
        k = kbf_sc[pl.ds(r0, tq), :]
        v = vbf_sc[pl.ds(r0, tq), :]
        for g in range(2 * tq // rg):
            rows = pl.ds(g * rg, rg)
            s = _dot_nt(qq_sc[rows, :], k)
            if masked:
                row = (lax.broadcasted_iota(jnp.int32, s.shape, 0) + g * rg) & (tq - 1)
                colp = lax.broadcasted_iota(jnp.int32, s.shape, 1)
                s = jnp.where(colp <= row, s, -jnp.inf)
            m_prev = m_sc[rows, :]
            m_new = jnp.maximum(m_prev, jnp.max(s, axis=1, keepdims=True))
            alpha = jnp.exp2(m_prev - m_new)
            p = jnp.exp2(s - pltpu.repeat(m_new, rep, axis=1))
            l_sc[rows, :] = alpha * l_sc[rows, :] + jnp.sum(p, axis=1, keepdims=True)
            acc_sc[rows, :] = alpha * acc_sc[rows, :] + _dot(p.astype(BF16), v)
            m_sc[rows, :] = m_new

    def body(kb, carry):
        block(kb, False)
        return carry

    lax.fori_loop(0, qi, body, 0)
    block(qi, True)
    a = acc_sc[...] / l_sc[...]
    o = a[0:tq] - lam_ref[...] * a[tq:2 * tq]
    rms = lax.rsqrt(jnp.mean(o * o, axis=-1, keepdims=True) + 1e-5)
    o_ref[...] = (o * rms * g_ref[...] * out_scale).astype(o_ref.dtype)


def _diff_attn(proj, lam, lam_init, subln_g, *, tq=512, rg=1024):
    t = proj.shape[0]
    tq = min(tq, t)
    rg = min(rg, 2 * tq)
    assert tq & (tq - 1) == 0
    w = 2 * DF_DH
    cq, ck, cv = 4 * BR_W // w, 5 * BR_W // w, 6 * BR_W // w
    vec = pl.BlockSpec((1, w), lambda h, i: (0, 0))
    return pl.pallas_call(
        functools.partial(_dattn_kernel, tq=tq, rg=rg, out_scale=1.0 - lam_init),
        out_shape=jax.ShapeDtypeStruct((t, BR_W), BF16),
        grid=(DF_HEADS, t // tq),
        in_specs=[pl.BlockSpec((tq, w), lambda h, i: (i, cq + h)),
                  pl.BlockSpec((t, w), lambda h, i: (0, ck + h)),
                  pl.BlockSpec((t, w), lambda h, i: (0, cv + h)),
                  vec, vec],
        out_specs=pl.BlockSpec((tq, w), lambda h, i: (i, h)),
        scratch_shapes=[pltpu.VMEM((t, w), BF16), pltpu.VMEM((t, w), BF16), pltpu.VMEM((2 * tq, w), BF16),
                        pltpu.VMEM((2 * tq, w), F32), pltpu.VMEM((2 * tq, w), F32),
                        pltpu.VMEM((2 * tq, w), F32)],
        compiler_params=_cparams("arbitrary", "arbitrary"),
        name="diff_attn",
    )(proj, proj, proj, jnp.full((1, w), lam, F32), subln_g.reshape(1, w))


def _rw_prep_kernel(r_ref, k_ref, v_ref, lo_ref, rp_ref, kp_ref, vp_ref, lop_ref,
                    mur_ref, muk_ref, muv_ref, mulo_ref, w0_ref, a0_ref, kkw_ref, ka_ref, rk_ref,
                    w2_ref, a2_ref, g2_ref, bd_ref,
                    r_out, lw_out, k_out, v_out, kk_out, b_out, g_out, bonus_out):
    first = pl.program_id(0) == 0

    def shifted(x_ref, xp_ref, mu_ref):
        x = x_ref[...]
        row = lax.broadcasted_iota(jnp.int32, x.shape, 0)
        last_prev = jnp.where(first, 0.0, xp_ref[7:8, :])
        prev = jnp.where(row == 0, last_prev, pltpu.roll(x, 1, 0))
        return x + (prev - x) * mu_ref[...]

    r = shifted(r_ref, rp_ref, mur_ref)
    k = shifted(k_ref, kp_ref, muk_ref)
    v = shifted(v_ref, vp_ref, muv_ref)
    lo = shifted(lo_ref, lop_ref, mulo_ref)
    wd = _dot(jnp.tanh(lo).astype(BF16), w2_ref[...])
    ad = _dot(lo.astype(BF16), a2_ref[...])
    g = _dot(_sigmoid(lo).astype(BF16), g2_ref[...])
    w_log = -_softplus(-(w0_ref[...] + wd)) - 0.5
    lw_out[...] = -jnp.exp(w_log)
    a = _sigmoid(a0_ref[...] + ad)
    bd = bd_ref[...]
    kk = k * kkw_ref[...]
    ss_hi, ss_mid, _ = _split3(kk * kk)
    ss = _dot(ss_hi, bd) + _dot(ss_mid, bd)
    kk = kk / jnp.maximum(jnp.sqrt(ss), 1e-12)
    kmod = k * (1.0 + (a - 1.0) * ka_ref[...])
    rkr = r * kmod * rk_ref[...]
    rk_hi, rk_mid, _ = _split3(rkr)
    bonus_out[...] = (_dot(rk_hi, bd) + _dot(rk_mid, bd)) * v
    r_out[...] = r
    k_out[...] = kmod
    v_out[...] = v
    kk_out[...] = kk
    b_out[...] = kk * a
    g_out[...] = g


def _rw_prep(proj, tail, shift_mu, w0, w2, a0, a2, g2, k_k, k_a, r_k, *, tb=256):
    t = proj.shape[0]
    tb = min(tb, t)
    c = BR_W
    nb8 = tb // 8

    def cur(ncol, blk):
        return pl.BlockSpec((tb, ncol), lambda i: (i, blk))

    def prv(ncol, blk):
        return pl.BlockSpec((8, ncol), lambda i: (jnp.maximum(i * nb8 - 1, 0), blk))

    vec = pl.BlockSpec((1, c), lambda i: (0, 0))
    veclo = pl.BlockSpec((1, RW_LORA_PAD), lambda i: (0, 0))
    wlo = pl.BlockSpec((RW_LORA_PAD, c), lambda i: (0, 0))
    out = pl.BlockSpec((tb, c), lambda i: (i, 0))

    def pad_rows(wm, off):
        return jnp.zeros((RW_LORA_PAD, c), F32).at[off:off + wm.shape[0]].set(wm).astype(BF16)

    mu_lo = jnp.zeros((RW_LORA_PAD,), F32).at[:RW_LORA_COLS].set(shift_mu[3 * c:])
    hid = np.arange(c) // RW_N
    bd = jnp.asarray((hid[:, None] == hid[None, :]).astype(np.float32), BF16)
    row = lambda a: a.reshape(1, -1)
    cb = COL_RW // c
    return pl.pallas_call(
        _rw_prep_kernel,
        out_shape=tuple(jax.ShapeDtypeStruct((t, c), F32) for _ in range(8)),
        grid=(t // tb,),
        in_specs=[cur(c, cb), cur(c, cb + 1), cur(c, cb + 2), cur(RW_LORA_PAD, BR_W // RW_LORA_PAD),
                  prv(c, cb), prv(c, cb + 1), prv(c, cb + 2), prv(RW_LORA_PAD, BR_W // RW_LORA_PAD),
                  vec, vec, vec, veclo, vec, vec, vec, vec, vec, wlo, wlo, wlo,
                  pl.BlockSpec((c, c), lambda i: (0, 0))],
        out_specs=tuple(out for _ in range(8)),
        compiler_params=_cparams("arbitrary"),
        name="rwkv7_prep",
    )(proj, proj, proj, tail, proj, proj, proj, tail,
      row(shift_mu[:c]), row(shift_mu[c:2 * c]), row(shift_mu[2 * c:3 * c]), row(mu_lo),
      row(w0), row(a0), row(k_k), row(k_a), row(r_k),
      pad_rows(w2, 0), pad_rows(a2, RW_DECAY_LORA), pad_rows(g2, RW_DECAY_LORA + RW_AAA_LORA), bd)


def _rw_masks():
    c = RW_CHUNK
    r = np.arange(c)
    tri = r[None, :] <= r[:, None]
    strict = r[None, :] < r[:, None]
    same16 = (r[:, None] // 16) == (r[None, :] // 16)
    same32 = (r[:, None] // 32) == (r[None, :] // 32)
    m = np.stack([tri, strict, same16, same32 & ~same16, ~same32, np.eye(c, dtype=bool)], axis=0)
    return m.astype(np.float32)


def _rw_rec_kernel(r_ref, lw_ref, k_ref, v_ref, kk_ref, b_ref, bonus_ref, g_ref, lg_ref, lb_ref, msk_ref,
                   o_ref, st_ref):
    c = RW_CHUNK
    n = RW_N
    npair = RW_HEADS // 2
    w = 2 * n

    @pl.when(pl.program_id(0) == 0)
    def _():
        st_ref[...] = jnp.zeros_like(st_ref)

    tri = msk_ref[0]
    lower = tri > 0.0
    strict = msk_ref[1] > 0.0
    in16 = msk_ref[2] > 0.0
    in32 = msk_ref[3] > 0.0
    off32 = msk_ref[4] > 0.0
    eye = msk_ref[5]
    tri_b = jnp.broadcast_to(tri.astype(BF16), (npair, c, c))
    bf = lambda x: x.astype(BF16)
    pairs = lambda ref: jnp.stack([ref[:, p * w:(p + 1) * w] for p in range(npair)], axis=0)
    first = lax.broadcasted_iota(jnp.int32, (1, 1, w), 2) < n
    ri = lax.broadcasted_iota(jnp.int32, (w, w), 0) < n
    ci = lax.broadcasted_iota(jnp.int32, (w, w), 1) < n
    same_head = ri == ci

    def per_head(m, x):
        res = _bdot(bf(m), bf(jnp.concatenate([x, x], axis=0)))
        return jnp.where(first, res[:npair], res[npair:])

    lw = pairs(lw_ref)
    r = pairs(r_ref)
    k = pairs(k_ref)
    v = pairs(v_ref)
    a = -pairs(kk_ref)
    b = pairs(b_ref)
    lw_hi, lw_mid, lw_lo = _split3(lw)
    cs = _bdot(tri_b, lw_hi) + _bdot(tri_b, lw_mid) + _bdot(tri_b, lw_lo)
    e_inc = jnp.exp(cs)
    e_exc = jnp.exp(cs - lw)
    e_neg = jnp.exp(-cs)
    ar = jnp.concatenate([a * e_exc, r * e_inc], axis=1)
    ar_h = bf(jnp.concatenate([jnp.where(first, ar, 0.0), jnp.where(first, 0.0, ar)], axis=0))
    bt = bf(b * e_neg)
    kt = bf(k * e_neg)
    pb = _bdot_nt(ar_h, jnp.concatenate([bt, bt], axis=0))
    pk = _bdot_nt(ar_h, jnp.concatenate([kt, kt], axis=0))
    a_ab = jnp.where(strict, pb[:, 0:c], 0.0)
    a_rb = jnp.where(lower, pb[:, c:2 * c], 0.0)
    a_ak = jnp.where(strict, pk[:, 0:c], 0.0)
    a_rk = jnp.where(lower, pk[:, c:2 * c], 0.0)
    ld = jnp.where(in16, a_ab, 0.0)
    t1 = eye + ld
    l2 = _bdot(bf(ld), bf(ld))
    t2 = t1 + _bdot(bf(t1), bf(l2))
    l4 = _bdot(bf(l2), bf(l2))
    t3 = t2 + _bdot(bf(t2), bf(l4))
    l8 = _bdot(bf(l4), bf(l4))
    td = t3 + _bdot(bf(t3), bf(l8))
    lo1 = jnp.where(in32, a_ab, 0.0)
    t32 = td + _bdot(bf(_bdot(bf(td), bf(lo1))), bf(td))
    lo2 = jnp.where(off32, a_ab, 0.0)
    tinv = t32 + _bdot(bf(_bdot(bf(t32), bf(lo2))), bf(t32))
    st = st_ref[...]
    arh = _bdot_nt(bf(ar), bf(st))
    u = per_head(tinv, arh[:, 0:c] + per_head(a_ak, v))
    o = arh[:, c:2 * c] + per_head(a_rb, u) + per_head(a_rk, v)
    c_last = cs[:, c - 1:c, :]
    e_tail = jnp.exp(c_last - cs)
    uv = jnp.concatenate([bf(u), bf(v)], axis=1)
    bk = jnp.concatenate([bf(b * e_tail), bf(k * e_tail)], axis=1)
    st_ref[...] = jnp.exp(c_last) * st + jnp.where(same_head, _bdot_tn(uv, bk), 0.0)

    def head_mean(x):
        s0 = jnp.sum(jnp.where(first, x, 0.0), axis=-1, keepdims=True)
        s1 = jnp.sum(jnp.where(first, 0.0, x), axis=-1, keepdims=True)
        return jnp.where(first, s0, s1) * (1.0 / n)

    oc = o - head_mean(o)
    on = oc * lax.rsqrt(head_mean(oc * oc) + RW_LNX_EPS)
    for p in range(npair):
        cols = slice(p * w, (p + 1) * w)
        y = (on[p] * lg_ref[:, cols] + lb_ref[:, cols] + bonus_ref[:, cols]) * g_ref[:, cols]
        o_ref[:, cols] = y.astype(o_ref.dtype)


def _rw_recurrence(r, lw, k, v, kk, b, bonus, g, lnx_g, lnx_b):
    t, cw = r.shape
    c = RW_CHUNK
    msk = _rw_masks()
    blk = pl.BlockSpec((c, cw), lambda i: (i, 0))
    vec = pl.BlockSpec((1, cw), lambda i: (0, 0))
    return pl.pallas_call(
        _rw_rec_kernel,
        out_shape=jax.ShapeDtypeStruct((t, cw), BF16),
        grid=(t // c,),
        in_specs=[blk] * 8 + [vec, vec, pl.BlockSpec(msk.shape, lambda i: (0, 0, 0))],
        out_specs=blk,
        scratch_shapes=[pltpu.VMEM((RW_HEADS // 2, 2 * RW_N, 2 * RW_N), F32)],
        compiler_params=_cparams("arbitrary"),
        name="rwkv7_recurrence",
    )(r, lw, k, v, kk, b, bonus, g, lnx_g.reshape(1, cw), lnx_b.reshape(1, cw), jnp.asarray(msk, F32))


def _rwkv7(proj, tail, shift_mu, w0, w2, a0, a2, g2, k_k, k_a, r_k, lnx_g, lnx_b):
    r, lw, k, v, kk, b, g, bonus = _rw_prep(proj, tail, shift_mu, w0, w2, a0, a2, g2, k_k, k_a,
                                            r_k.reshape(-1))
    return _rw_recurrence(r, lw, k, v, kk, b, bonus, g, lnx_g, lnx_b)


def _xattn_kernel(q_ref, kv_ref, o_ref):
    for h in range(XA_HEADS):
        q = q_ref[:, h * XA_DH:(h + 1) * XA_DH] * (XA_DH ** -0.5)
        k = kv_ref[:, h * XA_DH:(h + 1) * XA_DH]
        v = kv_ref[:, BR_W + h * XA_DH:BR_W + (h + 1) * XA_DH]
        s = _dot_nt(q.astype(BF16), k.astype(BF16))
        p = jnp.exp(s - jnp.max(s, axis=-1, keepdims=True))
        o = _dot(p.astype(BF16), v.astype(BF16)) / jnp.sum(p, axis=-1, keepdims=True)
        o_ref[:, h * XA_DH:(h + 1) * XA_DH] = o.astype(o_ref.dtype)


def _mem_attn(tail, mem_kv, *, tq=512):
    t = tail.shape[0]
    tq = min(tq, t)
    return pl.pallas_call(
        _xattn_kernel,
        out_shape=jax.ShapeDtypeStruct((t, BR_W), BF16),
        grid=(t // tq,),
        in_specs=[pl.BlockSpec((tq, BR_W), lambda i: (i, 0)),
                  pl.BlockSpec(mem_kv.shape, lambda i: (0, 0))],
        out_specs=pl.BlockSpec((tq, BR_W), lambda i: (i, 0)),
        compiler_params=_cparams("arbitrary"),
        name="mem_attn",
    )(tail, mem_kv)


def _merge_kernel(o0_ref, o1_ref, o2_ref, o3_ref, gd_ref, wbr_ref, wg0_ref, wg1_ref, wg2_ref, wg3_ref,
                  bg0_ref, bg1_ref, bg2_ref, bg3_ref, out_ref, wbr_bf, wg_bf):
    wg_refs = (wg0_ref, wg1_ref, wg2_ref, wg3_ref)

    @pl.when(pl.program_id(1) == 0)
    def _():
        wbr_bf[...] = wbr_ref[...].astype(BF16)
        for n in range(4):
            wg_bf[n] = wg_refs[n][...].astype(BF16)

    gd = gd_ref[...].astype(BF16)
    acc = None
    for n, (o_ref, bg_ref) in enumerate(zip((o0_ref, o1_ref, o2_ref, o3_ref),
                                            (bg0_ref, bg1_ref, bg2_ref, bg3_ref))):
        gate = _sigmoid(_dot(gd, wg_bf[n]) + bg_ref[...])
        term = gate * _dot(o_ref[...], wbr_bf[n])
        acc = term if acc is None else acc + term
    out_ref[...] = acc.astype(out_ref.dtype)


def _merge(branches, tail, w_br, w_gate_up, b_gate, layer, *, tm=512, tn=512):
    t = tail.shape[0]
    tm = min(tm, t)
    d = D_MODEL
    nj = d // tn
    ob = pl.BlockSpec((tm, BR_W), lambda j, i: (i, 0))
    gd_blk = (BR_W + RW_LORA_PAD) // GATE_RANK
    wg = [pl.BlockSpec((None, GATE_RANK, tn), lambda j, i, n=n: (layer, 0, n * nj + j)) for n in range(4)]
    bg = [pl.BlockSpec((None, 1, tn), lambda j, i, n=n: (layer, 0, n * nj + j)) for n in range(4)]
    bgate3 = b_gate.reshape(DEPTH, 1, 4 * d)
    return pl.pallas_call(
        _merge_kernel,
        out_shape=jax.ShapeDtypeStruct((t, d), BF16),
        grid=(nj, t // tm),
        in_specs=[ob, ob, ob, ob,
                  pl.BlockSpec((tm, GATE_RANK), lambda j, i: (i, gd_blk)),
                  pl.BlockSpec((None, 4, BR_W, tn), lambda j, i: (layer, 0, 0, j))] + wg + bg,
        out_specs=pl.BlockSpec((tm, tn), lambda j, i: (i, j)),
        scratch_shapes=[pltpu.VMEM((4, BR_W, tn), BF16), pltpu.VMEM((4, GATE_RANK, tn), BF16)],
        compiler_params=_cparams("arbitrary", "arbitrary"),
        name="gated_merge",
    )(*branches, tail, w_br, w_gate_up, w_gate_up, w_gate_up, w_gate_up, bgate3, bgate3, bgate3, bgate3)


def _router_kernel(x_ref, w_ref, b_ref, g_ref):
    xh, xm, _ = _split3(x_ref[...])
    wh, wm, _ = _split3(w_ref[...])
    logits = _dot(xh, wh) + _dot(xh, wm) + _dot(xm, wh) + b_ref[...]
    lane = lax.broadcasted_iota(jnp.int32, logits.shape, 1)
    work = jnp.where(lane < N_EXPERTS, logits, -jnp.inf)
    top = jnp.max(work, axis=-1, keepdims=True)
    gates = jnp.zeros_like(logits)
    denom = jnp.zeros_like(top)
    for _ in range(TOP_K):
        cur = jnp.max(work, axis=-1, keepdims=True)
        idx = jnp.min(jnp.where(work == cur, lane, 2 * 128), axis=-1, keepdims=True)
        sel = lane == idx
        e = jnp.exp(cur - top)
        gates = jnp.where(sel, e, gates)
        denom = denom + e
        work = jnp.where(sel, -jnp.inf, work)
    g_ref[...] = gates / denom


def _router(x, router_w, router_b, *, tm=512):
    t, d = x.shape
    tm = min(tm, t)
    wpad = jnp.zeros((d, 128), F32).at[:, :N_EXPERTS].set(router_w)
    bpad = jnp.zeros((1, 128), F32).at[0, :N_EXPERTS].set(router_b)
    return pl.pallas_call(
        _router_kernel,
        out_shape=jax.ShapeDtypeStruct((t, 128), F32),
        grid=(t // tm,),
        in_specs=[pl.BlockSpec((tm, d), lambda i: (i, 0)),
                  pl.BlockSpec((d, 128), lambda i: (0, 0)),
                  pl.BlockSpec((1, 128), lambda i: (0, 0))],
        out_specs=pl.BlockSpec((tm, 128), lambda i: (i, 0)),
        compiler_params=_cparams("arbitrary"),
        name="moe_router",
    )(x, wpad, bpad)


def _w1_prep_kernel(w_ref, p_ref, o_ref):
    o_ref[...] = _dot(w_ref[...].astype(BF16), p_ref[...]).astype(BF16)


def _w1_prep(exp_w1, layer, *, tk=1024):
    e, d, f2 = exp_w1.shape[1:]
    perm = np.concatenate([np.arange(0, f2, 2), np.arange(1, f2, 2)])
    pmat = np.zeros((f2, f2), np.float32)
    pmat[perm, np.arange(f2)] = 1.0
    return pl.pallas_call(
        _w1_prep_kernel,
        out_shape=jax.ShapeDtypeStruct((e, d, f2), BF16),
        grid=(e, d // tk),
        in_specs=[pl.BlockSpec((None, None, tk, f2), lambda a, b: (layer, a, b, 0)),
                  pl.BlockSpec((f2, f2), lambda a, b: (0, 0))],
        out_specs=pl.BlockSpec((None, tk, f2), lambda a, b: (a, b, 0)),
        compiler_params=_cparams("arbitrary", "arbitrary"),
        name="moe_w1_prep",
    )(exp_w1, jnp.asarray(pmat, BF16))


def _moe_kernel(x_ref, g_ref, w1_ref, b1_ref, w2_ref, b2_ref, o_ref):
    e = pl.program_id(1)
    gates = g_ref[...]

    @pl.when(e == 0)
    def _():
        gh, gm, _ = _split3(gates)
        bh, bm, _ = _split3(b2_ref[...])
        o_ref[...] = _dot(gh, bh) + _dot(gh, bm) + _dot(gm, bh)

    hid = _dot(x_ref[...], w1_ref[...]) + b1_ref[...]
    x_glu = jnp.minimum(hid[:, :EXPERT_FF], SWIGLU_LIMIT)
    x_lin = jnp.clip(hid[:, EXPERT_FF:], -SWIGLU_LIMIT, SWIGLU_LIMIT)
    act = x_glu * _sigmoid(SWIGLU_ALPHA * x_glu) * (x_lin + 1.0)
    lane = lax.broadcasted_iota(jnp.int32, gates.shape, 1)
    ge = jnp.sum(jnp.where(lane == e, gates, 0.0), axis=-1, keepdims=True)
    o_ref[...] += _dot((act * ge).astype(BF16), w2_ref[...].astype(BF16))


def _moe(x_bf, gates, w1p, b1p, exp_w2, exp_b2, layer, *, tm=512):
    t, d = x_bf.shape
    tm = min(tm, t)
    f2 = 2 * EXPERT_FF
    b2pad = jnp.zeros((128, d), F32).at[:N_EXPERTS].set(exp_b2[layer])
    return pl.pallas_call(
        _moe_kernel,
        out_shape=jax.ShapeDtypeStruct((t, d), F32),
        grid=(t // tm, N_EXPERTS),
        in_specs=[pl.BlockSpec((tm, d), lambda i, e: (i, 0)),
                  pl.BlockSpec((tm, 128), lambda i, e: (i, 0)),
                  pl.BlockSpec((None, d, f2), lambda i, e: (e, 0, 0)),
                  pl.BlockSpec((None, 1, f2), lambda i, e: (e, 0, 0)),
                  pl.BlockSpec((None, None, EXPERT_FF, d), lambda i, e: (layer, e, 0, 0)),
                  pl.BlockSpec((128, d), lambda i, e: (0, 0))],
        out_specs=pl.BlockSpec((tm, d), lambda i, e: (i, 0)),
        compiler_params=_cparams("arbitrary", "arbitrary"),
        name="moe_experts",
    )(x_bf, gates, w1p, b1p, exp_w2, b2pad)


def _tail_weight_t(w_in_t, layer):
    rows = lambda a, b: lax.slice(w_in_t, (layer, a, 0), (layer + 1, b, w_in_t.shape[2]))[0]
    lora = jnp.pad(rows(COL_LORA, COL_XQ), ((0, RW_LORA_PAD - RW_LORA_COLS), (0, 0)))
    return jnp.concatenate([rows(COL_XQ, COL_GDOWN), lora, rows(COL_GDOWN, IN_COLS)], axis=0)


def kernel(x, mem, mem_ln_g, mem_ln_b, w_in, hg_lb_raw, hg_norm_g, df_lam_q1, df_lam_k1, df_lam_q2, df_lam_k2, df_subln_g, rw_shift_mu, rw_w0, rw_w2, rw_a0, rw_a2, rw_g2, rw_k_k, rw_k_a, rw_r_k, rw_lnx_g, rw_lnx_b, w_mem_kv, w_br, w_gate_up, b_gate, w_o, ln1_g, ln1_b, router_w, router_b, exp_w1, exp_b1, exp_w2, exp_b2, ln2_g, ln2_b):
    bsz, t, d = x.shape
    assert bsz == 1
    xf = x.reshape(t, d)
    xb = xf.astype(BF16)
    _, memn = _layer_norm(mem.reshape(-1, d), mem_ln_g, mem_ln_b)
    lb_cum = jnp.cumsum(jax.nn.softmax(hg_lb_raw.astype(F32), axis=0), axis=0)
    lb_all = lb_cum - lb_cum[:1]
    f2 = 2 * EXPERT_FF
    perm = np.concatenate([np.arange(0, f2, 2), np.arange(1, f2, 2)])
    w_in_t = jnp.swapaxes(w_in, 1, 2)
    for l in range(DEPTH):
        proj = _matmul_nt(xb, w_in_t, layer=l, nrows=MAIN_COLS, name="in_proj_main")
        tail = _matmul_nt(xb, _tail_weight_t(w_in_t, l), tn=TAIL_COLS // 2, name="in_proj_tail")
        o_hg = _hgrn2(proj, lb_all[l], hg_norm_g[l])
        lam_init = 0.8 - 0.6 * math.exp(-0.3 * l)
        lam = (jnp.exp(jnp.sum(df_lam_q1[l] * df_lam_k1[l]).astype(F32))
               - jnp.exp(jnp.sum(df_lam_q2[l] * df_lam_k2[l]).astype(F32)) + lam_init)
        o_df = _diff_attn(proj, lam, lam_init, df_subln_g[l])
        o_rw = _rwkv7(proj, tail, rw_shift_mu[l], rw_w0[l], rw_w2[l], rw_a0[l], rw_a2[l], rw_g2[l],
                      rw_k_k[l], rw_k_a[l], rw_r_k[l], rw_lnx_g[l], rw_lnx_b[l])
        mem_kv = _matmul(memn, w_mem_kv, layer=l, name="mem_kv")
        o_xa = _mem_attn(tail, mem_kv)
        merged = _merge((o_hg, o_df, o_rw, o_xa), tail, w_br, w_gate_up, b_gate, l)
        y = _matmul(merged, w_o, layer=l, name="out_proj")
        xf, xb = _residual_layer_norm(xf, y, ln1_g[l], ln1_b[l])
        gates = _router(xf, router_w[l], router_b[l])
        w1p = _w1_prep(exp_w1, l)
        b1p = exp_b1[l][:, perm].reshape(N_EXPERTS, 1, f2)
        y = _moe(xb, gates, w1p, b1p, exp_w2, exp_b2, l)
        xf, xb = _residual_layer_norm(xf, y, ln2_g[l], ln2_b[l])
    return xf.reshape(bsz, t, d)
```
